```python
import jax, jax.numpy as jnp
from jax import lax
import numpy as np

D_MODEL = 4096
BATCH = 1
SEQ = 8192
DEPTH = 2

CTX_LEN = 256
GRID_W = 64
W_S5 = D_MODEL // 2
S5_P = 16
S5_G = W_S5 // S5_P
S5_N = 64
S5_MAX_RE = -1e-4
W_HG = D_MODEL // 2
HG_DK = 128
HG_H = W_HG // HG_DK
HG_DV = W_HG // HG_H
CHUNK = 64
D_FF = -(-8 * D_MODEL // (3 * 256)) * 256
N_IN = W_S5 + 5 * W_HG + 2 * D_MODEL
SPLITS = (W_S5, W_S5 + W_HG, W_S5 + 2 * W_HG, W_S5 + 3 * W_HG, W_S5 + 4 * W_HG, W_S5 + 5 * W_HG)
EPS = 1e-6

kernel_name = "hybrid_s5_hgrn2_dit_block"


def rmsnorm(x, g):
    xf = x.astype(jnp.float32)
    y = xf * lax.rsqrt(jnp.mean(xf * xf, axis=-1, keepdims=True) + EPS)
    return (y * g.astype(jnp.float32)).astype(x.dtype)


def modulate(h, shift, scale):
    return h * (1.0 + scale) + shift


def orient(t, direction):
    return t if direction == 0 else jnp.flip(t, axis=1)


def ssm_combine(e1, e2):
    a1r, a1i, b1r, b1i = e1
    a2r, a2i, b2r, b2i = e2
    ar = a2r * a1r - a2i * a1i
    ai = a2r * a1i + a2i * a1r
    br = a2r * b1r - a2i * b1i + b2r
    bi = a2r * b1i + a2i * b1r + b2i
    return (ar, ai, br, bi)


def s5_discretise(lam_re, lam_im, log_dt, b_re, b_im):
    lr = jnp.minimum(lam_re.astype(jnp.float32), S5_MAX_RE)
    li = lam_im.astype(jnp.float32)
    dt = jnp.exp(log_dt.astype(jnp.float32))[:, None]
    mag = jnp.exp(lr * dt)
    ar = mag * jnp.cos(li * dt)
    ai = mag * jnp.sin(li * dt)
    den = lr * lr + li * li
    nr = ar - 1.0
    zr = (nr * lr + ai * li) / den
    zi = (ai * lr - nr * li) / den
    br = b_re.astype(jnp.float32)
    bi = b_im.astype(jnp.float32)
    bbr = zr[..., None] * br - zi[..., None] * bi
    bbi = zr[..., None] * bi + zi[..., None] * br
    return ar, ai, bbr, bbi


def s5_states(u, ar, ai, bbr, bbi, h0r, h0i):
    xr = jnp.einsum('btgp,gnp->btgn', u, bbr)
    xi = jnp.einsum('btgp,gnp->btgn', u, bbi)
    xr = xr.at[:, 0].add(ar * h0r - ai * h0i)
    xi = xi.at[:, 0].add(ar * h0i + ai * h0r)
    _, _, hr, hi = lax.associative_scan(
        ssm_combine, (jnp.broadcast_to(ar, xr.shape), jnp.broadcast_to(ai, xi.shape), xr, xi), axis=1)
    return hr, hi


def s5_readout(hr, hi, c_re, c_im):
    return jnp.einsum('btgn,gpn->btgp', hr, c_re) - jnp.einsum('btgn,gpn->btgp', hi, c_im)


def s5_branch(uc, ul, lam_re, lam_im, log_dt, b_re, b_im, c_re, c_im, d_skip, w_glu, b_glu, need_ctx):
    dtype = ul.dtype
    bsz = ul.shape[0]

    def to_groups(u):
        return u.astype(jnp.float32).reshape(u.shape[0], u.shape[1], S5_G, S5_P)

    uc_g, ul_g = to_groups(uc), to_groups(ul)
    zeros = jnp.zeros((bsz, S5_G, S5_N), jnp.float32)
    d = d_skip.astype(jnp.float32)
    yl = ul_g * d
    yc = uc_g * d if need_ctx else None
    for direction in range(2):
        ar, ai, bbr, bbi = s5_discretise(lam_re[direction], lam_im[direction], log_dt[direction],
                                         b_re[direction], b_im[direction])
        cr = c_re[direction].astype(jnp.float32)
        ci = c_im[direction].astype(jnp.float32)
        hcr, hci = s5_states(orient(uc_g, direction), ar, ai, bbr, bbi, zeros, zeros)
        hlr, hli = s5_states(orient(ul_g, direction), ar, ai, bbr, bbi, hcr[:, -1], hci[:, -1])
        yl = yl + orient(s5_readout(hlr, hli, cr, ci), direction)
        if need_ctx:
            yc = yc + orient(s5_readout(hcr, hci, cr, ci), direction)

    def glu(y):
        z = jax.nn.gelu(y.reshape(y.shape[0], y.shape[1], W_S5)).astype(dtype)
        return z * jax.nn.sigmoid(z @ w_glu + b_glu)

    return (glu(yc) if need_ctx else None), glu(yl)


def hg_heads(t):
    return t.reshape(t.shape[0], t.shape[1], HG_H, t.shape[-1] // HG_H)


def hgrn2_chunked(q, k, v, g, s0):
    bsz, t_len = q.shape[0], q.shape[1]
    n_chunks = t_len // CHUNK

    def to_chunks(t):
        return t.reshape(bsz, n_chunks, CHUNK, HG_H, t.shape[-1]).transpose(1, 0, 3, 2, 4)

    incl = jnp.tril(jnp.ones((CHUNK, CHUNK), dtype=bool))[:, :, None]

    def step(s, inp):
        qc, kc, vc, gc = inp
        G = jnp.cumsum(gc, axis=2)
        G_last = G[:, :, -1]
        o_inter = jnp.einsum('bhik,bhkv->bhiv', qc * jnp.exp(G), s)
        diff = G[:, :, :, None, :] - G[:, :, None, :, :]
        decay = jnp.where(incl, jnp.exp(jnp.where(incl, diff, 0.0)), 0.0)
        scores = jnp.einsum('bhik,bhjk,bhijk->bhij', qc, kc, decay)
        o = o_inter + jnp.einsum('bhij,bhjv->bhiv', scores, vc)
        s_new = jnp.exp(G_last)[..., None] * s + jnp.einsum(
            'bhjk,bhjv->bhkv', kc * jnp.exp(G_last[:, :, None] - G), vc)
        return s_new, o

    s_fin, o = lax.scan(step, s0, (to_chunks(q), to_chunks(k), to_chunks(v), to_chunks(g)))
    o = o.transpose(1, 0, 3, 2, 4).reshape(bsz, t_len, HG_H, HG_DV)
    return o, s_fin


def hgrn2_final_state(k, v, g):
    G = jnp.cumsum(g, axis=1)
    return jnp.einsum('bthk,bthv->bhkv', k * jnp.exp(G[:, -1:] - G), v)


def hgrn2_branch(qc, vc, zfc, zbc, ogc, ql, vl, zfl, zbl, ogl, lb, norm_g, need_ctx):
    dtype = ql.dtype
    bsz = ql.shape[0]

    def gates(z, lb_d):
        f = lb_d + (1.0 - lb_d) * jax.nn.sigmoid(z.astype(jnp.float32))
        return hg_heads(1.0 - f), hg_heads(jnp.log(f))

    qc_h, vc_h = hg_heads(qc.astype(jnp.float32)), hg_heads(vc.astype(jnp.float32))
    ql_h, vl_h = hg_heads(ql.astype(jnp.float32)), hg_heads(vl.astype(jnp.float32))
    ol = jnp.zeros(ql_h.shape[:3] + (HG_DV,), jnp.float32)
    oc = jnp.zeros(qc_h.shape[:3] + (HG_DV,), jnp.float32) if need_ctx else None
    for direction, (zc, zl) in enumerate(((zfc, zfl), (zbc, zbl))):
        kc, gc = gates(zc, lb[direction])
        kl, gl = gates(zl, lb[direction])
        if need_ctx:
            s0 = jnp.zeros((bsz, HG_H, HG_DK, HG_DV), jnp.float32)
            oc_d, s_ctx = hgrn2_chunked(orient(qc_h, direction), orient(kc, direction),
                                        orient(vc_h, direction), orient(gc, direction), s0)
            oc = oc + orient(oc_d, direction)
        else:
            s_ctx = hgrn2_final_state(orient(kc, direction), orient(vc_h, direction), orient(gc, direction))
        ol_d, _ = hgrn2_chunked(orient(ql_h, direction), orient(kl, direction),
                                orient(vl_h, direction), orient(gl, direction), s_ctx)
        ol = ol + orient(ol_d, direction)

    def readout(o, og):
        o = o * lax.rsqrt(jnp.mean(o * o, axis=-1, keepdims=True) + EPS)
        o = o.reshape(o.shape[0], o.shape[1], W_HG) * norm_g.astype(jnp.float32)
        return (o * jax.nn.silu(og.astype(jnp.float32))).astype(dtype)

    return (readout(oc, ogc) if need_ctx else None), readout(ol, ogl)


def hybrid_mixer(pc, pl, need_ctx, lam_re, lam_im, log_dt, b_re, b_im, c_re, c_im, d_skip,
                 w_glu, b_glu, lb, hg_g, w_pa, w_pb, b_gate, w_o):
    uc, qc, vc, zfc, zbc, ogc, glc = jnp.split(pc, SPLITS, axis=-1)
    ul, ql, vl, zfl, zbl, ogl, gll = jnp.split(pl, SPLITS, axis=-1)
    ya_c, ya_l = s5_branch(uc, ul, lam_re, lam_im, log_dt, b_re, b_im, c_re, c_im, d_skip,
                           w_glu, b_glu, need_ctx)
    yb_c, yb_l = hgrn2_branch(qc, vc, zfc, zbc, ogc, ql, vl, zfl, zbl, ogl, lb, hg_g, need_ctx)

    def merge(ya, yb, glog):
        gts = jax.nn.sigmoid((glog + b_gate).astype(jnp.float32)).astype(ya.dtype)
        ga, gb = jnp.split(gts, 2, axis=-1)
        return (ga * (ya @ w_pa) + gb * (yb @ w_pb)) @ w_o

    out_l = merge(ya_l, yb_l, gll)
    out_c = merge(ya_c, yb_c, glc) if need_ctx else None
    return out_c, out_l


def swiglu(h, w_in, w_out):
    gt, up = jnp.split(h @ w_in, 2, axis=-1)
    return (jax.nn.silu(gt) * up) @ w_out


def setup_inputs(seed: int = 0) -> dict:
    key = jax.random.key(seed)
    ks = jax.random.split(key, 32)
    f32 = jnp.float32

    def nrm(k, shape, scale):
        return jax.random.normal(k, shape, f32) * scale

    lam_im_init = jnp.pi * jnp.arange(S5_N, dtype=f32)
    return {
        "x": nrm(ks[0], (BATCH, SEQ, D_MODEL), 1.0),
        "c": nrm(ks[1], (BATCH, D_MODEL), 1.0),
        "ctx": nrm(ks[2], (BATCH, CTX_LEN, D_MODEL), 1.0),
        "c_ctx": nrm(ks[3], (D_MODEL,), 1.0),
        "w_ada": nrm(ks[4], (DEPTH, D_MODEL, 6 * D_MODEL), 0.5 * D_MODEL ** -0.5),
        "b_ada": nrm(ks[5], (DEPTH, 6 * D_MODEL), 0.01),
        "norm_g": 1.0 + nrm(ks[6], (DEPTH, 2, D_MODEL), 0.01),
        "w_in": nrm(ks[7], (DEPTH, D_MODEL, N_IN), D_MODEL ** -0.5),
        "b_gate": nrm(ks[8], (DEPTH, 2 * D_MODEL), 0.01),
        "s5_lam_re": -0.5 + nrm(ks[9], (DEPTH, 2, S5_G, S5_N), 0.01),
        "s5_lam_im": lam_im_init + nrm(ks[10], (DEPTH, 2, S5_G, S5_N), 0.01),
        "s5_log_dt": jax.random.uniform(ks[11], (DEPTH, 2, S5_G), f32, np.log(1e-3), np.log(1e-1)),
        "s5_b_re": nrm(ks[12], (DEPTH, 2, S5_G, S5_N, S5_P), (2.0 * S5_P) ** -0.5),
        "s5_b_im": nrm(ks[13], (DEPTH, 2, S5_G, S5_N, S5_P), (2.0 * S5_P) ** -0.5),
        "s5_c_re": nrm(ks[14], (DEPTH, 2, S5_G, S5_P, S5_N), (2.0 * S5_N) ** -0.5),
        "s5_c_im": nrm(ks[15], (DEPTH, 2, S5_G, S5_P, S5_N), (2.0 * S5_N) ** -0.5),
        "s5_d": nrm(ks[16], (DEPTH, S5_G, S5_P), 1.0),
        "w_glu": nrm(ks[17], (DEPTH, W_S5, W_S5), W_S5 ** -0.5),
        "b_glu": nrm(ks[18], (DEPTH, W_S5), 0.01),
        "hg_lb_logits": nrm(ks[19], (2, DEPTH, W_HG), 0.5),
        "hg_norm_g": 1.0 + nrm(ks[20], (DEPTH, W_HG), 0.01),
        "w_pa": nrm(ks[21], (DEPTH, W_S5, D_MODEL), W_S5 ** -0.5),
        "w_pb": nrm(ks[22], (DEPTH, W_HG, D_MODEL), W_HG ** -0.5),
        "w_o": nrm(ks[23], (DEPTH, D_MODEL, D_MODEL), D_MODEL ** -0.5),
        "w_ffn_in": nrm(ks[24], (DEPTH, D_MODEL, 2 * D_FF), D_MODEL ** -0.5),
        "w_ffn_out": nrm(ks[25], (DEPTH, D_FF, D_MODEL), D_FF ** -0.5),
        "final_g": 1.0 + nrm(ks[26], (D_MODEL,), 0.01),
    }


def reference(x, c, ctx, c_ctx, w_ada, b_ada, norm_g, w_in, b_gate, s5_lam_re, s5_lam_im,
              s5_log_dt, s5_b_re, s5_b_im, s5_c_re, s5_c_im, s5_d, w_glu, b_glu, hg_lb_logits,
              hg_norm_g, w_pa, w_pb, w_o, w_ffn_in, w_ffn_out, final_g):
    ctx_len = ctx.shape[1]
    lb_all = jnp.cumsum(jax.nn.softmax(hg_lb_logits.astype(jnp.float32), axis=1), axis=1)
    lb_all = lb_all - lb_all[:, :1]
    silu_c = jax.nn.silu(c)
    silu_cc = jax.nn.silu(c_ctx)
    for l in range(DEPTH):
        need_ctx = l < DEPTH - 1
        sh_m, sc_m, g_m, sh_f, sc_f, g_f = [m[:, None] for m in jnp.split(silu_c @ w_ada[l] + b_ada[l], 6, axis=-1)]
        ch_m, cs_m, cg_m, ch_f, cs_f, cg_f = jnp.split(silu_cc @ w_ada[l] + b_ada[l], 6, axis=-1)
        hl = modulate(rmsnorm(x, norm_g[l, 0]), sh_m, sc_m)
        hc = modulate(rmsnorm(ctx, norm_g[l, 0]), ch_m, cs_m)
        p = jnp.concatenate([hc, hl], axis=1) @ w_in[l]
        out_c, out_l = hybrid_mixer(p[:, :ctx_len], p[:, ctx_len:], need_ctx,
                                    s5_lam_re[l], s5_lam_im[l], s5_log_dt[l], s5_b_re[l], s5_b_im[l],
                                    s5_c_re[l], s5_c_im[l], s5_d[l], w_glu[l], b_glu[l],
                                    lb_all[:, l], hg_norm_g[l], w_pa[l], w_pb[l], b_gate[l], w_o[l])
        x = x + g_m * out_l
        x = x + g_f * swiglu(modulate(rmsnorm(x, norm_g[l, 1]), sh_f, sc_f), w_ffn_in[l], w_ffn_out[l])
        if need_ctx:
            ctx = ctx + cg_m * out_c
            ctx = ctx + cg_f * swiglu(modulate(rmsnorm(ctx, norm_g[l, 1]), ch_f, cs_f),
                                      w_ffn_in[l], w_ffn_out[l])
    return rmsnorm(x, final_g)
```

```python
import functools

import jax
import jax.numpy as jnp
from jax import lax
from jax.experimental import pallas as pl
from jax.experimental.pallas import tpu as pltpu

BF16 = jnp.bfloat16
F32 = jnp.float32
HIGHEST = lax.Precision.HIGHEST

EPS = 1e-6
S5_P = 16
S5_N = 64
S5_MAX_RE = -1e-4
S5_L = 16
S5_GB = 8
S5_CB = 16
HG_DK = 128
HG_C = 64
HG_SUB = 8
VMEM_LIMIT_BYTES = 56 * 1024 * 1024


def _params(n_axes):
    return pltpu.CompilerParams(dimension_semantics=("arbitrary",) * n_axes,
                                vmem_limit_bytes=VMEM_LIMIT_BYTES)


def _mm_kernel(a_ref, b_ref, o_ref):
    o_ref[...] = jnp.dot(a_ref[...].astype(BF16), b_ref[...].astype(BF16),
                         preferred_element_type=F32).astype(o_ref.dtype)


def _mm_acc_kernel(a_ref, b_ref, o_ref):
    part = jnp.dot(a_ref[...], b_ref[...], preferred_element_type=F32)

    @pl.when(pl.program_id(2) == 0)
    def _():
        o_ref[...] = part

    @pl.when(pl.program_id(2) > 0)
    def _():
        o_ref[...] += part


def _matmul(a, b, *, tm, tn, tk=None, out_dtype=F32, name="matmul"):
    m, k = a.shape
    n = b.shape[1]
    assert m % tm == 0 and n % tn == 0
    if tk is None:
        return pl.pallas_call(
            _mm_kernel, grid=(m // tm, n // tn),
            in_specs=[pl.BlockSpec((tm, k), lambda i, j: (i, 0)),
                      pl.BlockSpec((k, tn), lambda i, j: (0, j))],
            out_specs=pl.BlockSpec((tm, tn), lambda i, j: (i, j)),
            out_shape=jax.ShapeDtypeStruct((m, n), out_dtype),
            compiler_params=_params(2), name=name)(a, b)
    assert k % tk == 0 and out_dtype == F32
    return pl.pallas_call(
        _mm_acc_kernel, grid=(m // tm, n // tn, k // tk),
        in_specs=[pl.BlockSpec((tm, tk), lambda i, j, kk: (i, kk)),
                  pl.BlockSpec((tk, tn), lambda i, j, kk: (kk, j))],
        out_specs=pl.BlockSpec((tm, tn), lambda i, j, kk: (i, j)),
        out_shape=jax.ShapeDtypeStruct((m, n), F32),
        compiler_params=_params(3), name=name)(a, b)


def _s5_prepare(lam_re, lam_im, log_dt, b_re, b_im, c_re, c_im):
    L = S5_L
    lr = jnp.minimum(lam_re.astype(F32), S5_MAX_RE)
    li = lam_im.astype(F32)
    dt = jnp.exp(log_dt.astype(F32))[..., None]
    mag = jnp.exp(lr * dt)
    ar = mag * jnp.cos(li * dt)
    ai = mag * jnp.sin(li * dt)
    den = lr * lr + li * li
    nr = ar - 1.0
    zr = (nr * lr + ai * li) / den
    zi = (ai * lr - nr * li) / den
    br = b_re.astype(F32)
    bi = b_im.astype(F32)
    bbr = zr[..., None] * br - zi[..., None] * bi
    bbi = zr[..., None] * bi + zi[..., None] * br
    ks = jnp.arange(L + 1, dtype=F32)[:, None, None, None]
    pmag = jnp.exp(ks * lr * dt)
    pr = pmag * jnp.cos(ks * li * dt)
    pi = pmag * jnp.sin(ks * li * dt)
    mr = pr[..., None] * bbr - pi[..., None] * bbi
    mi = pr[..., None] * bbi + pi[..., None] * bbr
    cr = c_re.astype(F32)
    ci = c_im.astype(F32)
    kk = (jnp.einsum('dgpn,kdgnq->kdgpq', cr, mr, precision=HIGHEST)
          - jnp.einsum('dgpn,kdgnq->kdgpq', ci, mi, precision=HIGHEST))
    kz = kk.at[L].set(0.0)
    s_idx = jnp.arange(L)[:, None]
    r_idx = jnp.arange(L)[None, :]
    idx_f = jnp.where(r_idx >= s_idx, r_idx - s_idx, L)
    idx_b = jnp.where(s_idx >= r_idx, s_idx - r_idx, L)
    g = lam_re.shape[1]
    lp = L * S5_P

    def toeplitz(idx, d):
        t = kz[idx, d]
        return t.transpose(2, 0, 4, 1, 3).reshape(g, lp, lp)

    toe = toeplitz(idx_f, 0) + toeplitz(idx_b, 1)

    def state_w(m, d, order):
        w = m[order, d]
        return w.transpose(1, 0, 3, 2).reshape(g, lp, S5_N)

    fwd_order = jnp.arange(L - 1, -1, -1)
    bwd_order = jnp.arange(L)
    w = jnp.concatenate([toe, state_w(mr, 0, fwd_order), state_w(mi, 0, fwd_order),
                         state_w(mr, 1, bwd_order), state_w(mi, 1, bwd_order)], axis=-1)

    def readout(d, powers):
        ppr = pr[powers, d][:, :, None, :]
        ppi = pi[powers, d][:, :, None, :]
        cpr = cr[d][None] * ppr - ci[d][None] * ppi
        cpi = cr[d][None] * ppi + ci[d][None] * ppr
        top = cpr.transpose(1, 3, 0, 2).reshape(g, S5_N, lp)
        bot = (-cpi).transpose(1, 3, 0, 2).reshape(g, S5_N, lp)
        return jnp.concatenate([top, bot], axis=1)

    rf = readout(0, jnp.arange(1, L + 1))
    rb = readout(1, jnp.arange(L, 0, -1))
    a = jnp.stack([jnp.concatenate([pr[L, 0], pr[L, 0]], -1), jnp.concatenate([-pi[L, 0], pi[L, 0]], -1),
                   jnp.concatenate([pr[L, 1], pr[L, 1]], -1), jnp.concatenate([-pi[L, 1], pi[L, 1]], -1)])
    return w.astype(BF16), rf.astype(BF16), rb.astype(BF16), a


def _s5_intra_kernel(u_ref, w_ref, y1_ref, sf_ref, sb_ref):
    lp = u_ref.shape[2]
    n2 = 2 * S5_N
    for j in range(u_ref.shape[0]):
        res = jnp.dot(u_ref[j], w_ref[j], preferred_element_type=F32)
        y1_ref[j] = res[:, :lp]
        sf_ref[:, j * n2:(j + 1) * n2] = res[:, lp:lp + n2]
        sb_ref[:, j * n2:(j + 1) * n2] = res[:, lp + n2:]


def _s5_carry_kernel(sf_ref, sb_ref, a_ref, hf_ref, hb_ref, st_ref):
    @pl.when(pl.program_id(0) == 0)
    def _():
        st_ref[...] = jnp.zeros_like(st_ref)

    a1f, a2f, a1b, a2b = a_ref[0], a_ref[1], a_ref[2], a_ref[3]
    cb = sf_ref.shape[0]
    hf = st_ref[0]
    hb = st_ref[1]
    for c in range(cb):
        hf_ref[c] = hf.astype(BF16)
        hf = a1f * hf + a2f * pltpu.roll(hf, S5_N, 1) + sf_ref[c]
    for c in range(cb - 1, -1, -1):
        hb_ref[c] = hb.astype(BF16)
        hb = a1b * hb + a2b * pltpu.roll(hb, S5_N, 1) + sb_ref[c]
    st_ref[0] = hf
    st_ref[1] = hb


def _s5_readout_kernel(y1_ref, hf_ref, hb_ref, rf_ref, rb_ref, y_ref):
    n2 = 2 * S5_N
    for j in range(y1_ref.shape[0]):
        y_ref[j] = (y1_ref[j]
                    + jnp.dot(hf_ref[:, j * n2:(j + 1) * n2], rf_ref[j], preferred_element_type=F32)
                    + jnp.dot(hb_ref[:, j * n2:(j + 1) * n2], rb_ref[j], preferred_element_type=F32))


def _s5_scan(u, n_lat, w, rf, rb, a):
    t, width = u.shape
    g = width // S5_P
    L, gb, cb = S5_L, S5_GB, S5_CB
    nc = t // L
    lp = L * S5_P
    n2 = 2 * S5_N
    assert t % (L * cb) == 0 and n_lat % (L * cb) == 0 and g % gb == 0
    ut = u.astype(BF16).reshape(nc, L, g, S5_P).transpose(2, 0, 1, 3).reshape(g, nc, lp)
    y1, sf, sb = pl.pallas_call(
        _s5_intra_kernel, grid=(g // gb,),
        in_specs=[pl.BlockSpec((gb, nc, lp), lambda i: (i, 0, 0)),
                  pl.BlockSpec((gb, lp, lp + 2 * n2), lambda i: (i, 0, 0))],
        out_specs=[pl.BlockSpec((gb, nc, lp), lambda i: (i, 0, 0)),
                   pl.BlockSpec((nc, gb * n2), lambda i: (0, i)),
                   pl.BlockSpec((nc, gb * n2), lambda i: (0, i))],
        out_shape=[jax.ShapeDtypeStruct((g, nc, lp), F32),
                   jax.ShapeDtypeStruct((nc, g * n2), F32),
                   jax.ShapeDtypeStruct((nc, g * n2), F32)],
        compiler_params=_params(1), name="s5_intra")(ut, w)

    nb = nc // cb
    nb_lat = n_lat // (L * cb)
    nb_ctx = nb - nb_lat
    fwd_blk = lambda s: (jnp.where(s < nb_ctx, nb_lat + s, s - nb_ctx), 0, 0)
    bwd_blk = lambda s: (nb - 1 - s, 0, 0)
    hf, hb = pl.pallas_call(
        _s5_carry_kernel, grid=(nb,),
        in_specs=[pl.BlockSpec((cb, g, n2), fwd_blk),
                  pl.BlockSpec((cb, g, n2), bwd_blk),
                  pl.BlockSpec((4, g, n2), lambda s: (0, 0, 0))],
        out_specs=[pl.BlockSpec((cb, g, n2), fwd_blk),
                   pl.BlockSpec((cb, g, n2), bwd_blk)],
        out_shape=[jax.ShapeDtypeStruct((nc, g, n2), BF16),
                   jax.ShapeDtypeStruct((nc, g, n2), BF16)],
        scratch_shapes=[pltpu.VMEM((2, g, n2), F32)],
        compiler_params=_params(1), name="s5_carry")(
            sf.reshape(nc, g, n2), sb.reshape(nc, g, n2), a)

    y = pl.pallas_call(
        _s5_readout_kernel, grid=(g // gb,),
        in_specs=[pl.BlockSpec((gb, nc, lp), lambda i: (i, 0, 0)),
                  pl.BlockSpec((nc, gb * n2), lambda i: (0, i)),
                  pl.BlockSpec((nc, gb * n2), lambda i: (0, i)),
                  pl.BlockSpec((gb, n2, lp), lambda i: (i, 0, 0)),
                  pl.BlockSpec((gb, n2, lp), lambda i: (i, 0, 0))],
        out_specs=pl.BlockSpec((gb, nc, lp), lambda i: (i, 0, 0)),
        out_shape=jax.ShapeDtypeStruct((g, nc, lp), F32),
        compiler_params=_params(1), name="s5_readout")(
            y1, hf.reshape(nc, g * n2), hb.reshape(nc, g * n2), rf, rb)
    return y.reshape(g, nc, L, S5_P).transpose(1, 2, 0, 3).reshape(t, width)


def _hgrn2_kernel(q_ref, v_ref, z_ref, lb_ref, o_ref, st_ref, *, rev):
    c, sub = HG_C, HG_SUB
    nt = c // sub

    @pl.when(pl.program_id(0) == 0)
    def _():
        st_ref[...] = jnp.zeros_like(st_ref)

    row = lax.broadcasted_iota(jnp.int32, (c, c), 0)
    col = lax.broadcasted_iota(jnp.int32, (c, c), 1)
    prow = (c - 1 - row) if rev else row
    pcol = (c - 1 - col) if rev else col
    causal = pcol <= prow
    tri = causal.astype(F32)
    diag_mask = jnp.logical_and(causal, (prow >> 3) == (pcol >> 3))
    levels = (1, 2, 4)
    lvl_masks = [(prow >> (4 + i)) == (pcol >> (4 + i)) for i in range(len(levels))]
    er = lax.broadcasted_iota(jnp.int32, (sub * HG_DK, c), 0) >> 7
    ec = lax.broadcasted_iota(jnp.int32, (sub * HG_DK, c), 1) & (sub - 1)
    spread = (er == ec).astype(BF16)
    last_sub = 0 if rev else sub - 1

    def head(h, carry):
        sl = pl.ds(pl.multiple_of(h * HG_DK, HG_DK), HG_DK)
        q = q_ref[:, sl]
        v = v_ref[:, sl]
        z = z_ref[:, sl]
        lb = lb_ref[:, sl]
        f = lb + (1.0 - lb) * jax.nn.sigmoid(z)
        k = 1.0 - f
        g = jnp.log(f)
        gin = jnp.dot(tri, g, preferred_element_type=F32, precision=HIGHEST)
        g_tiles = [gin[a * sub:(a + 1) * sub] for a in range(nt)]
        q_tiles = [q[a * sub:(a + 1) * sub] for a in range(nt)]
        k_tiles = [k[a * sub:(a + 1) * sub] for a in range(nt)]
        last_rows = [gin[a * sub + last_sub:a * sub + last_sub + 1] for a in range(nt)]
        g_total = last_rows[0] if rev else last_rows[nt - 1]
        st = st_ref[h]
        vb = v.astype(BF16)

        qe = (q * jnp.exp(gin)).astype(BF16)
        o = lax.dot_general(qe, st.astype(BF16), (((1,), (1,)), ((), ())), preferred_element_type=F32)

        g3 = gin.reshape(nt, sub, HG_DK)
        k3 = k.reshape(nt, sub, HG_DK)
        slabs = []
        for d in range(sub):
            g_ref = jnp.broadcast_to(g3[:, d:d + 1, :], (nt, sub, HG_DK)).reshape(c, HG_DK)
            k_ref = jnp.broadcast_to(k3[:, d:d + 1, :], (nt, sub, HG_DK)).reshape(c, HG_DK)
            slabs.append((q * k_ref * jnp.exp(jnp.minimum(gin - g_ref, 0.0))).astype(BF16))
        scores = jnp.where(diag_mask,
                           jnp.dot(jnp.concatenate(slabs, axis=1), spread, preferred_element_type=F32), 0.0)

        for lvl, half in enumerate(levels):
            qh, kh = [], []
            for a in range(nt):
                pa = nt - 1 - a if rev else a
                blk = pa // (2 * half)
                pb = blk * 2 * half + half - 1
                g_bnd = last_rows[nt - 1 - pb if rev else pb]
                if (pa // half) % 2 == 1:
                    qh.append(q_tiles[a] * jnp.exp(g_tiles[a] - g_bnd))
                    kh.append(jnp.zeros((sub, HG_DK), F32))
                else:
                    qh.append(jnp.zeros((sub, HG_DK), F32))
                    kh.append(k_tiles[a] * jnp.exp(g_bnd - g_tiles[a]))
            s_l = lax.dot_general(jnp.concatenate(qh, axis=0).astype(BF16),
                                  jnp.concatenate(kh, axis=0).astype(BF16),
                                  (((1,), (1,)), ((), ())), preferred_element_type=F32)
            scores = scores + jnp.where(lvl_masks[lvl], s_l, 0.0)

        o = o + jnp.dot(scores.astype(BF16), vb, preferred_element_type=F32)
        o_ref[:, sl] = o

        kt = (k * jnp.exp(g_total - gin)).astype(BF16)
        st_ref[h] = st * jnp.exp(g_total) + lax.dot_general(
            vb, kt, (((0,), (0,)), ((), ())), preferred_element_type=F32)
        return carry

    lax.fori_loop(0, st_ref.shape[0], head, 0)


def _hgrn2_scan(p, n_lat, col_q, col_v, col_z, lb, *, rev):
    t = p.shape[0]
    w_hg = lb.shape[1]
    heads = w_hg // HG_DK
    c = HG_C
    n_steps = t // c
    n_lat_c = n_lat // c
    n_ctx_c = n_steps - n_lat_c
    if rev:
        chunk = lambda s: n_steps - 1 - s
    else:
        chunk = lambda s: jnp.where(s < n_ctx_c, n_lat_c + s, s - n_ctx_c)
    spec = lambda colblk: pl.BlockSpec((c, w_hg), lambda s: (chunk(s), colblk))
    return pl.pallas_call(
        functools.partial(_hgrn2_kernel, rev=rev), grid=(n_steps,),
        in_specs=[spec(col_q // w_hg), spec(col_v // w_hg), spec(col_z // w_hg),
                  pl.BlockSpec((1, w_hg), lambda s: (0, 0))],
        out_specs=pl.BlockSpec((c, w_hg), lambda s: (chunk(s), 0)),
        out_shape=jax.ShapeDtypeStruct((t, w_hg), F32),
        scratch_shapes=[pltpu.VMEM((heads, HG_DK, HG_DK), F32)],
        compiler_params=_params(1), name="hgrn2_bwd" if rev else "hgrn2_fwd")(p, p, p, lb)


def _rmsnorm(x, g):
    return x * lax.rsqrt(jnp.mean(x * x, axis=-1, keepdims=True) + EPS) * g


def kernel(x, c, ctx, c_ctx, w_ada, b_ada, norm_g, w_in, b_gate, s5_lam_re, s5_lam_im, s5_log_dt, s5_b_re, s5_b_im, s5_c_re, s5_c_im, s5_d, w_glu, b_glu, hg_lb_logits, hg_norm_g, w_pa, w_pb, w_o, w_ffn_in, w_ffn_out, final_g):
    assert x.shape[0] == 1
    depth = w_in.shape[0]
    n_lat, d_model = x.shape[1], x.shape[2]
    n_ctx = ctx.shape[1]
    t = n_lat + n_ctx
    w_s5 = w_glu.shape[1]
    w_hg = hg_norm_g.shape[1]
    d_ff = w_ffn_out.shape[1]
    col_q, col_v, col_zf, col_zb, col_og, col_gate = (w_s5, w_s5 + w_hg, w_s5 + 2 * w_hg, w_s5 + 3 * w_hg,
                                                      w_s5 + 4 * w_hg, w_s5 + 5 * w_hg)
    tm = t // 11 if t % 11 == 0 else t

    lb_all = jnp.cumsum(jax.nn.softmax(hg_lb_logits.astype(F32), axis=1), axis=1)
    lb_all = lb_all - lb_all[:, :1]
    xa = jnp.concatenate([x[0], ctx[0]], axis=0).astype(F32)
    is_lat = (jnp.arange(t) < n_lat)[:, None]
    cond = jnp.zeros((16, d_model), F32).at[0].set(jax.nn.silu(c[0])).at[1].set(jax.nn.silu(c_ctx))

    def rows(m):
        return jnp.where(is_lat, m[0][None], m[1][None])

    for l in range(depth):
        mod = _matmul(cond.astype(BF16), w_ada[l], tm=16, tn=1024, name="adaln") + b_ada[l][None]
        sh_m, sc_m, g_m, sh_f, sc_f, g_f = [rows(m) for m in jnp.split(mod, 6, axis=-1)]

        h = (_rmsnorm(xa, norm_g[l, 0]) * (1.0 + sc_m) + sh_m).astype(BF16)
        p = _matmul(h, w_in[l].astype(BF16), tm=tm, tn=1024, name="w_in")

        u = p[:, :w_s5]
        s5_ops = _s5_prepare(s5_lam_re[l], s5_lam_im[l], s5_log_dt[l], s5_b_re[l], s5_b_im[l],
                             s5_c_re[l], s5_c_im[l])
        y = _s5_scan(u, n_lat, *s5_ops) + u * s5_d[l].reshape(1, w_s5)
        z = jax.nn.gelu(y)
        gate = _matmul(z.astype(BF16), w_glu[l].astype(BF16), tm=tm, tn=1024, name="w_glu") + b_glu[l][None]
        ya = (z * jax.nn.sigmoid(gate)).astype(BF16)

        o = (_hgrn2_scan(p, n_lat, col_q, col_v, col_zf, lb_all[0, l][None], rev=False)
             + _hgrn2_scan(p, n_lat, col_q, col_v, col_zb, lb_all[1, l][None], rev=True))
        o = o.reshape(t, w_hg // HG_DK, HG_DK)
        o = o * lax.rsqrt(jnp.mean(o * o, axis=-1, keepdims=True) + EPS)
        o = o.reshape(t, w_hg) * hg_norm_g[l][None]
        yb = (o * jax.nn.silu(p[:, col_og:col_og + w_hg])).astype(BF16)

        gts = jax.nn.sigmoid(p[:, col_gate:] + b_gate[l][None])
        merged = (gts[:, :d_model] * _matmul(ya, w_pa[l].astype(BF16), tm=tm, tn=1024, name="w_pa")
                  + gts[:, d_model:] * _matmul(yb, w_pb[l].astype(BF16), tm=tm, tn=1024, name="w_pb"))
        xa = xa + g_m * _matmul(merged.astype(BF16), w_o[l].astype(BF16), tm=tm, tn=1024, name="w_o")

        h = (_rmsnorm(xa, norm_g[l, 1]) * (1.0 + sc_f) + sh_f).astype(BF16)
        hh = _matmul(h, w_ffn_in[l].astype(BF16), tm=tm, tn=512, name="w_ffn_in")
        act = (jax.nn.silu(hh[:, :d_ff]) * hh[:, d_ff:]).astype(BF16)
        xa = xa + g_f * _matmul(act, w_ffn_out[l].astype(BF16), tm=tm, tn=1024, tk=d_ff // 2,
                                name="w_ffn_out")

    return _rmsnorm(xa[:n_lat], final_g)[None].astype(x.dtype)
```
